```python
import jax, jax.numpy as jnp
from jax import lax
import numpy as np


D_MODEL = 2048
BATCH = 4
SEQ = 8192
DEPTH = 4

EPS = 1e-6
CHUNK = 128
GMLP_WIDTH = D_MODEL // 4
GMLP_GROUPS = 4
GMLP_GROUP_DIM = GMLP_WIDTH // GMLP_GROUPS
ATTN_WIDTH = D_MODEL // 2
HEAD_DIM = 64
N_Q_HEADS = ATTN_WIDTH // HEAD_DIM
N_KV_HEADS = max(1, N_Q_HEADS // 8)
GQA_GROUP = N_Q_HEADS // N_KV_HEADS
WINDOW = 128
ROT_DIM = HEAD_DIM // 4
ROPE_THETA = 500000.0
MLSTM_WIDTH = D_MODEL - GMLP_WIDTH - ATTN_WIDTH
MLSTM_HEADS = 4
MLSTM_HEAD_DIM = MLSTM_WIDTH // MLSTM_HEADS
MLSTM_CHUNK = 128
CONV_WIDTH = 4
FORGET_BIAS = 3.0
D_FF = ((8 * D_MODEL + 3 * 256 - 1) // (3 * 256)) * 256
PROJ_SPLITS = (GMLP_WIDTH, GMLP_WIDTH, ATTN_WIDTH, N_KV_HEADS * HEAD_DIM, N_KV_HEADS * HEAD_DIM,
               2 * MLSTM_WIDTH, MLSTM_WIDTH, MLSTM_WIDTH, MLSTM_HEADS, MLSTM_HEADS)
IN_PROJ_WIDTH = sum(PROJ_SPLITS)

kernel_name = 'hybrid_parallel_heads_gmlp_swa_mlstm'


def rmsnorm(x, g):
    xf = x.astype(jnp.float32)
    y = xf * lax.rsqrt(jnp.mean(xf * xf, axis=-1, keepdims=True) + EPS)
    return (y * g.astype(jnp.float32)).astype(x.dtype)


def rotary_tables(seq_len):
    inv_freq = ROPE_THETA ** (-jnp.arange(0, ROT_DIM, 2, dtype=jnp.float32) / ROT_DIM)
    ang = jnp.arange(seq_len, dtype=jnp.float32)[:, None] * inv_freq[None, :]
    return jnp.cos(ang), jnp.sin(ang)


def partial_rotary(x, cos, sin):
    half = ROT_DIM // 2
    xr, xp = x[..., :ROT_DIM], x[..., ROT_DIM:]
    x1, x2 = xr[..., :half], xr[..., half:]
    c = cos[None, :, None, :].astype(x.dtype)
    s = sin[None, :, None, :].astype(x.dtype)
    return jnp.concatenate([x1 * c - x2 * s, x2 * c + x1 * s, xp], axis=-1)


def chunked_spatial_gating(u, v, v_gain, w_s, b_s):
    b, s, _ = v.shape
    nc = s // CHUNK
    u = jax.nn.gelu(u)
    v = rmsnorm(jax.nn.gelu(v), v_gain)
    vc = v.reshape(b, nc, CHUNK, GMLP_GROUPS, GMLP_GROUP_DIM)
    tril = jnp.tril(jnp.ones((CHUNK, CHUNK), dtype=bool))
    w = jnp.where(tril[None], w_s, jnp.zeros_like(w_s))
    mixed = jnp.einsum('gts,bnsgc->bntgc', w, vc) + b_s.T[None, None, :, :, None]
    return u * mixed.reshape(b, s, GMLP_WIDTH)


def sliding_window_attention(q, k, v, sinks):
    b, s, _, _ = q.shape
    blk = WINDOW
    nb = s // blk
    qb = q.reshape(b, nb, blk, N_KV_HEADS, GQA_GROUP, HEAD_DIM)
    kb = k.reshape(b, nb, blk, N_KV_HEADS, HEAD_DIM)
    vb = v.reshape(b, nb, blk, N_KV_HEADS, HEAD_DIM)
    prev = lambda t: jnp.concatenate([jnp.zeros_like(t[:, :1]), t[:, :-1]], axis=1)
    kk = jnp.concatenate([prev(kb), kb], axis=2)
    vv = jnp.concatenate([prev(vb), vb], axis=2)
    scores = jnp.einsum('bnqhgd,bnkhd->bnhgqk', qb, kk,
                        preferred_element_type=jnp.float32) * (HEAD_DIM ** -0.5)
    qi = jnp.arange(blk)[:, None]
    kj = jnp.arange(2 * blk)[None, :]
    band = (kj > qi) & (kj <= qi + blk)
    first_ok = (jnp.arange(nb)[:, None, None] > 0) | (kj >= blk)[None]
    mask = band[None] & first_ok
    scores = jnp.where(mask[None, :, None, None], scores, -jnp.inf)
    sink = sinks.astype(jnp.float32).reshape(N_KV_HEADS, GQA_GROUP)[None, None, :, :, None, None]
    mx = jnp.maximum(scores.max(axis=-1, keepdims=True), sink)
    p = jnp.exp(scores - mx)
    p = p / (p.sum(axis=-1, keepdims=True) + jnp.exp(sink - mx))
    o = jnp.einsum('bnhgqk,bnkhd->bnqhgd', p.astype(v.dtype), vv)
    return o.reshape(b, s, N_Q_HEADS * HEAD_DIM)


def causal_short_conv(x, w):
    s = x.shape[1]
    xp = jnp.pad(x, ((0, 0), (CONV_WIDTH - 1, 0), (0, 0)))
    out = xp[:, 0:s] * w[0]
    for j in range(1, CONV_WIDTH):
        out = out + xp[:, j:j + s] * w[j]
    return out


def mlstm(q, k, v, o_pre, i_pre, f_pre, i_bias, f_bias, head_gain):
    b, s, _ = q.shape
    L = MLSTM_CHUNK
    nc = s // L
    H, D = MLSTM_HEADS, MLSTM_HEAD_DIM

    def heads(t):
        return t.reshape(b, nc, L, H, D).transpose(1, 0, 3, 2, 4).astype(jnp.float32)

    def gates(t):
        return t.reshape(b, nc, L, H).transpose(1, 0, 3, 2)

    qh = heads(q) * (D ** -0.5)
    kh = heads(k)
    vh = heads(v)
    ig = gates((i_pre + i_bias).astype(jnp.float32))
    lf = gates(jax.nn.log_sigmoid((f_pre + f_bias).astype(jnp.float32)))
    causal = jnp.tril(jnp.ones((L, L), dtype=bool))

    def body(carry, inp):
        C, n, m = carry
        qc, kc, vc, ic, fc = inp
        bcum = jnp.cumsum(fc, axis=-1)
        logd = bcum[..., :, None] - bcum[..., None, :] + ic[..., None, :]
        logd = jnp.where(causal, logd, -jnp.inf)
        inter = bcum + m[..., None]
        m_t = jnp.maximum(inter, logd.max(axis=-1))
        sc = jnp.einsum('bhtd,bhsd->bhts', qc, kc) * jnp.exp(logd - m_t[..., None])
        w_inter = jnp.exp(inter - m_t)
        num = jnp.einsum('bhts,bhsd->bhtd', sc, vc) + w_inter[..., None] * jnp.einsum('bhtk,bhkv->bhtv', qc, C)
        den = sc.sum(axis=-1) + w_inter * jnp.einsum('bhtk,bhk->bht', qc, n)
        h = num / jnp.maximum(jnp.abs(den), jnp.exp(-m_t))[..., None]
        b_last = bcum[..., -1]
        logw = b_last[..., None] - bcum + ic
        m_new = jnp.maximum(b_last + m, logw.max(axis=-1))
        wk = jnp.exp(logw - m_new[..., None])
        decay = jnp.exp(b_last + m - m_new)
        C_new = decay[..., None, None] * C + jnp.einsum('bhs,bhsk,bhsv->bhkv', wk, kc, vc)
        n_new = decay[..., None] * n + jnp.einsum('bhs,bhsk->bhk', wk, kc)
        return (C_new, n_new, m_new), h

    init = (jnp.zeros((b, H, D, D), jnp.float32), jnp.zeros((b, H, D), jnp.float32),
            jnp.zeros((b, H), jnp.float32))
    _, hs = lax.scan(body, init, (qh, kh, vh, ig, lf))
    hs = hs.transpose(1, 0, 3, 2, 4).reshape(b, s, H, D)
    hs = rmsnorm(hs, head_gain.reshape(H, D)).reshape(b, s, MLSTM_WIDTH).astype(q.dtype)
    return jax.nn.sigmoid(o_pre) * hs


def hybrid_layer(x, cos, sin, norm1_g, w_in, gmlp_v_gain, gmlp_w_s, gmlp_b_s, attn_sinks,
                 mlstm_conv_w, mlstm_i_bias, mlstm_f_bias, mlstm_head_gain, w_out,
                 norm2_g, w_gate, w_up, w_down):
    b, s, _ = x.shape
    hn = rmsnorm(x, norm1_g)
    proj = hn @ w_in
    split_idx = np.cumsum(PROJ_SPLITS)[:-1].tolist()
    a_u, a_v, b_q, b_k, b_v, c_qk, c_v, c_o, c_i, c_f = jnp.split(proj, split_idx, axis=-1)
    mix_a = chunked_spatial_gating(a_u, a_v, gmlp_v_gain, gmlp_w_s, gmlp_b_s)
    q = partial_rotary(b_q.reshape(b, s, N_Q_HEADS, HEAD_DIM), cos, sin)
    k = partial_rotary(b_k.reshape(b, s, N_KV_HEADS, HEAD_DIM), cos, sin)
    v = b_v.reshape(b, s, N_KV_HEADS, HEAD_DIM)
    mix_b = sliding_window_attention(q, k, v, attn_sinks)
    c_qk = jax.nn.silu(causal_short_conv(c_qk, mlstm_conv_w))
    c_q, c_k = jnp.split(c_qk, 2, axis=-1)
    mix_c = mlstm(c_q, c_k, c_v, c_o, c_i, c_f, mlstm_i_bias, mlstm_f_bias, mlstm_head_gain)
    mixed = jnp.concatenate([mix_a, mix_b, mix_c], axis=-1)
    h = x + mixed @ w_out
    hn2 = rmsnorm(h, norm2_g)
    return h + (jax.nn.silu(hn2 @ w_gate) * (hn2 @ w_up)) @ w_down


def setup_inputs(seed: int = 0) -> dict:
    key = jax.random.key(seed)
    ks = jax.random.split(key, 17)
    f32 = jnp.float32
    nrm = lambda k, shape, scale: jax.random.normal(k, shape, f32) * scale
    return {
        'x': nrm(ks[0], (BATCH, SEQ, D_MODEL), 1.0),
        'norm1_g': 1.0 + nrm(ks[1], (DEPTH, D_MODEL), 0.02),
        'w_in': nrm(ks[2], (DEPTH, D_MODEL, IN_PROJ_WIDTH), D_MODEL ** -0.5),
        'gmlp_v_gain': 1.0 + nrm(ks[3], (DEPTH, GMLP_WIDTH), 0.02),
        'gmlp_w_s': nrm(ks[4], (DEPTH, GMLP_GROUPS, CHUNK, CHUNK), CHUNK ** -0.5),
        'gmlp_b_s': 1.0 + nrm(ks[5], (DEPTH, GMLP_GROUPS, CHUNK), 0.02),
        'attn_sinks': nrm(ks[6], (DEPTH, N_Q_HEADS), 0.5),
        'mlstm_conv_w': nrm(ks[7], (DEPTH, CONV_WIDTH, 2 * MLSTM_WIDTH), CONV_WIDTH ** -0.5),
        'mlstm_i_bias': nrm(ks[8], (DEPTH, MLSTM_HEADS), 0.1),
        'mlstm_f_bias': FORGET_BIAS + nrm(ks[9], (DEPTH, MLSTM_HEADS), 0.1),
        'mlstm_head_gain': 1.0 + nrm(ks[10], (DEPTH, MLSTM_WIDTH), 0.02),
        'w_out': nrm(ks[11], (DEPTH, D_MODEL, D_MODEL), D_MODEL ** -0.5),
        'norm2_g': 1.0 + nrm(ks[12], (DEPTH, D_MODEL), 0.02),
        'w_gate': nrm(ks[13], (DEPTH, D_MODEL, D_FF), D_MODEL ** -0.5),
        'w_up': nrm(ks[14], (DEPTH, D_MODEL, D_FF), D_MODEL ** -0.5),
        'w_down': nrm(ks[15], (DEPTH, D_FF, D_MODEL), D_FF ** -0.5),
        'final_g': 1.0 + nrm(ks[16], (D_MODEL,), 0.02),
    }


def reference(x, norm1_g, w_in, gmlp_v_gain, gmlp_w_s, gmlp_b_s, attn_sinks, mlstm_conv_w,
              mlstm_i_bias, mlstm_f_bias, mlstm_head_gain, w_out, norm2_g, w_gate, w_up,
              w_down, final_g):
    cos, sin = rotary_tables(x.shape[1])
    h = x
    for l in range(DEPTH):
        h = hybrid_layer(h, cos, sin, norm1_g[l], w_in[l], gmlp_v_gain[l], gmlp_w_s[l], gmlp_b_s[l],
                         attn_sinks[l], mlstm_conv_w[l], mlstm_i_bias[l], mlstm_f_bias[l],
                         mlstm_head_gain[l], w_out[l], norm2_g[l], w_gate[l], w_up[l], w_down[l])
    return rmsnorm(h, final_g)
```

```python
import functools

import jax
import jax.numpy as jnp
from jax import lax
from jax.experimental import pallas as pl
from jax.experimental.pallas import tpu as pltpu

F32 = jnp.float32
BF16 = jnp.bfloat16

D_MODEL = 2048
EPS = 1e-6
CHUNK = 128
GMLP_WIDTH = D_MODEL // 4
GMLP_GROUPS = 4
ATTN_WIDTH = D_MODEL // 2
HEAD_DIM = 64
N_Q_HEADS = ATTN_WIDTH // HEAD_DIM
N_KV_HEADS = 2
ROT_DIM = 16
ROPE_THETA = 500000.0
MLSTM_WIDTH = D_MODEL - GMLP_WIDTH - ATTN_WIDTH
MLSTM_HEADS = 4
MLSTM_HEAD_DIM = MLSTM_WIDTH // MLSTM_HEADS
CONV_WIDTH = 4
D_FF = 5632

OFF_UV = 0
OFF_Q = 2 * GMLP_WIDTH
OFF_KV = OFF_Q + ATTN_WIDTH
OFF_CQK = OFF_KV + 2 * N_KV_HEADS * HEAD_DIM
OFF_CVO = OFF_CQK + 2 * MLSTM_WIDTH
OFF_GATES = OFF_CVO + 2 * MLSTM_WIDTH
IN_PROJ_WIDTH = OFF_GATES + 2 * MLSTM_HEADS
LANES = 128
IN_PROJ_PADDED = OFF_GATES + LANES

VMEM_LIMIT_BYTES = 56 * 1024 * 1024


def _rms(x):
    return x * lax.rsqrt(jnp.mean(x * x, axis=-1, keepdims=True) + EPS)


def _in_proj_kernel(x_ref, g_ref, w_ref, uv_ref, q_ref, kv_ref, cqk_ref, cvo_ref, gates_ref):
    hn = (_rms(x_ref[...]) * g_ref[...]).astype(BF16)

    def mm(lo, hi):
        return jnp.dot(hn, w_ref[:, lo:hi], preferred_element_type=F32)

    uv_ref[...] = mm(OFF_UV, OFF_Q).astype(BF16)
    q_ref[...] = mm(OFF_Q, OFF_KV).astype(BF16)
    kv_ref[...] = mm(OFF_KV, OFF_CQK).astype(BF16)
    cqk_ref[...] = mm(OFF_CQK, OFF_CVO).astype(BF16)
    cvo_ref[...] = mm(OFF_CVO, OFF_GATES).astype(BF16)
    gates_ref[...] = mm(OFF_GATES, IN_PROJ_PADDED)


def _in_proj(x2d, g, w_pad, tm):
    m = x2d.shape[0]
    widths = (OFF_Q - OFF_UV, OFF_KV - OFF_Q, OFF_CQK - OFF_KV, OFF_CVO - OFF_CQK, OFF_GATES - OFF_CVO)
    out_shape = [jax.ShapeDtypeStruct((m, w), BF16) for w in widths]
    out_shape.append(jax.ShapeDtypeStruct((m, LANES), F32))
    out_specs = [pl.BlockSpec((tm, w), lambda i: (i, 0)) for w in widths]
    out_specs.append(pl.BlockSpec((tm, LANES), lambda i: (i, 0)))
    return pl.pallas_call(
        _in_proj_kernel,
        grid=(m // tm,),
        in_specs=[
            pl.BlockSpec((tm, D_MODEL), lambda i: (i, 0)),
            pl.BlockSpec((1, D_MODEL), lambda i: (0, 0)),
            pl.BlockSpec((D_MODEL, IN_PROJ_PADDED), lambda i: (0, 0), pipeline_mode=pl.Buffered(1)),
        ],
        out_specs=out_specs,
        out_shape=out_shape,
        compiler_params=pltpu.CompilerParams(
            dimension_semantics=("arbitrary",), vmem_limit_bytes=VMEM_LIMIT_BYTES),
        name="in_proj",
    )(x2d, g, w_pad)


def _mixer_kernel(uv_ref, q_ref, kv_ref, cqk_ref, cvo_ref, gates_ref, rc_ref, rs1_ref, rs2_ref,
                  vgain_ref, ws_ref, bst_ref, sinks_ref, convw_ref, gbias_ref, hgain_ref,
                  out_ref,
                  wtril_sc, bfull_sc, bdk_sc, bdv_sc, xbuf_sc, cn_sc, m_sc):
    n = pl.program_id(1)
    L = CHUNK
    row = lax.broadcasted_iota(jnp.int32, (L, L), 0)
    col = lax.broadcasted_iota(jnp.int32, (L, L), 1)
    causal = col <= row
    lo_half = col < HEAD_DIM
    neg_inf = jnp.float32(-jnp.inf)

    @pl.when(n == 0)
    def _start_of_sequence():
        for g in range(GMLP_GROUPS):
            wtril_sc[g] = jnp.where(causal, ws_ref[g], 0.0).astype(BF16)
            bfull_sc[g] = jnp.broadcast_to(bst_ref[:, g:g + 1], (L, L))
        r4 = lax.broadcasted_iota(jnp.int32, (4 * L, 2 * L), 0)
        c4 = lax.broadcasted_iota(jnp.int32, (4 * L, 2 * L), 1)
        ones_a = (c4 >= 2 * HEAD_DIM) & (c4 < 3 * HEAD_DIM) & (r4 < 2 * L)
        ones_b = (c4 >= 3 * HEAD_DIM) & (r4 >= 2 * L)
        ones_pat = jnp.where(ones_a | ones_b, 1.0, 0.0).astype(BF16)
        for h in range(N_KV_HEADS):
            bdk_sc[h] = jnp.zeros((4 * L, L), BF16)
            bdv_sc[h] = ones_pat
        xbuf_sc[0:8, :] = jnp.zeros((8, 2 * MLSTM_WIDTH), F32)
        cn_sc[...] = jnp.zeros(cn_sc.shape, F32)
        m_sc[...] = jnp.zeros(m_sc.shape, F32)

    uv = uv_ref[...].astype(F32)
    u = jax.nn.gelu(uv[:, :GMLP_WIDTH])
    v = _rms(jax.nn.gelu(uv[:, GMLP_WIDTH:])) * vgain_ref[...]
    vb = v.astype(BF16)
    for g in range(GMLP_GROUPS):
        sl = slice(g * L, (g + 1) * L)
        mixed = jnp.dot(wtril_sc[g], vb[:, sl], preferred_element_type=F32) + bfull_sc[g]
        out_ref[:, sl] = (u[:, sl] * mixed).astype(BF16)

    rc = rc_ref[...]
    rs1 = rs1_ref[...]
    rs2 = rs2_ref[...]

    def rotary(xs):
        return (xs * rc + pltpu.roll(xs, LANES - ROT_DIM // 2, 1) * rs1
                + pltpu.roll(xs, ROT_DIM // 2, 1) * rs2)

    for h in range(N_KV_HEADS):
        bdk_sc[h, 0:L, :] = bdk_sc[h, L:2 * L, :]
        bdk_sc[h, 2 * L:3 * L, :] = bdk_sc[h, 3 * L:4 * L, :]
        bdv_sc[h, 0:L, 0:L] = bdv_sc[h, L:2 * L, 0:L]
        bdv_sc[h, 2 * L:3 * L, 0:L] = bdv_sc[h, 3 * L:4 * L, 0:L]
    kvf = kv_ref[...].astype(F32)
    kr = rotary(kvf[:, :L])
    vf = kvf[:, L:]
    kroll = pltpu.roll(kr, HEAD_DIM, 1)
    vroll = pltpu.roll(vf, HEAD_DIM, 1)
    zero = jnp.zeros((L, L), F32)
    bdk_sc[0, L:2 * L, :] = jnp.where(lo_half, kr, zero).astype(BF16)
    bdk_sc[0, 3 * L:4 * L, :] = jnp.where(lo_half, zero, kroll).astype(BF16)
    bdk_sc[1, L:2 * L, :] = jnp.where(lo_half, kroll, zero).astype(BF16)
    bdk_sc[1, 3 * L:4 * L, :] = jnp.where(lo_half, zero, kr).astype(BF16)
    bdv_sc[0, L:2 * L, 0:L] = jnp.where(lo_half, vf, zero).astype(BF16)
    bdv_sc[0, 3 * L:4 * L, 0:L] = jnp.where(lo_half, zero, vroll).astype(BF16)
    bdv_sc[1, L:2 * L, 0:L] = jnp.where(lo_half, vroll, zero).astype(BF16)
    bdv_sc[1, 3 * L:4 * L, 0:L] = jnp.where(lo_half, zero, vf).astype(BF16)

    qi = lax.broadcasted_iota(jnp.int32, (L, 2 * L), 0)
    kj = lax.broadcasted_iota(jnp.int32, (L, 2 * L), 1)
    band = (kj > qi) & (kj <= qi + L) & ((kj >= L) | (n > 0))
    slabs_per_kv = N_Q_HEADS // N_KV_HEADS // 2
    for h in range(N_KV_HEADS):
        q_slabs = []
        for r in range(slabs_per_kv):
            j = slabs_per_kv * h + r
            qs = q_ref[:, j * L:(j + 1) * L].astype(F32)
            q_slabs.append((rotary(qs) * (HEAD_DIM ** -0.5)).astype(BF16))
        q_h = jnp.concatenate(q_slabs, axis=0)
        s_h = lax.dot_general(q_h, bdk_sc[h], (((1,), (1,)), ((), ())),
                              preferred_element_type=F32)
        p_rows = []
        e_sinks = []
        for r in range(slabs_per_kv):
            j = slabs_per_kv * h + r
            p_row = []
            for a in range(2):
                s = s_h[r * L:(r + 1) * L, a * 2 * L:(a + 1) * 2 * L]
                s = jnp.where(band, s, neg_inf)
                sink = sinks_ref[2 * j + a]
                mx = jnp.maximum(jnp.max(s, axis=1, keepdims=True), sink)
                p_row.append(jnp.exp(s - mx).astype(BF16))
                e_sinks.append(jnp.exp(sink - mx))
            p_rows.append(jnp.concatenate(p_row, axis=1))
        p_h = jnp.concatenate(p_rows, axis=0)
        o_h = jnp.dot(p_h, bdv_sc[h], preferred_element_type=F32)
        for r in range(slabs_per_kv):
            j = slabs_per_kv * h + r
            o = o_h[r * L:(r + 1) * L, 0:L]
            den = o_h[r * L:(r + 1) * L, L:2 * L] + jnp.where(lo_half, e_sinks[2 * r], e_sinks[2 * r + 1])
            out_ref[:, GMLP_WIDTH + j * L:GMLP_WIDTH + (j + 1) * L] = (o / den).astype(BF16)

    xbuf_sc[8:8 + L, :] = cqk_ref[...].astype(F32)
    base = 8 - (CONV_WIDTH - 1)
    conv = xbuf_sc[base:base + L, :] * convw_ref[0:1, :]
    for t in range(1, CONV_WIDTH):
        conv = conv + xbuf_sc[base + t:base + t + L, :] * convw_ref[t:t + 1, :]
    xbuf_sc[0:8, :] = xbuf_sc[L:L + 8, :]
    cqk = conv * jax.nn.sigmoid(conv)
    q_m = cqk[:, :MLSTM_WIDTH] * (MLSTM_HEAD_DIM ** -0.5)
    k_m = cqk[:, MLSTM_WIDTH:]

    gts = gates_ref[...] + gbias_ref[...]
    is_f = (col >= MLSTM_HEADS) & (col < 2 * MLSTM_HEADS)
    gts = jnp.where(is_f, jax.nn.log_sigmoid(gts), gts)
    g_rows = gts.T[0:8, :]
    lane8 = lax.broadcasted_iota(jnp.int32, (8, L), 1)
    c_rows = g_rows
    shift = 1
    while shift < L:
        c_rows = c_rows + jnp.where(lane8 >= shift, pltpu.roll(c_rows, shift, 1), 0.0)
        shift *= 2
    c_cols = jnp.concatenate([c_rows, jnp.zeros((L - 8, L), F32)], axis=0).T

    ones_col = jnp.where(col == 0, 1.0, 0.0).astype(BF16)
    cvo = cvo_ref[...]
    for h in range(MLSTM_HEADS):
        sl = slice(h * L, (h + 1) * L)
        bc = c_cols[:, MLSTM_HEADS + h:MLSTM_HEADS + h + 1]
        br = c_rows[MLSTM_HEADS + h:MLSTM_HEADS + h + 1, :]
        igr = g_rows[h:h + 1, :]
        m_prev = m_sc[h:h + 1, 0:1]
        q_h = q_m[:, sl]
        q_hb = q_h.astype(BF16)
        kt_h = k_m[:, sl].T
        v1_h = jnp.concatenate([cvo[:, sl], ones_col], axis=1)
        cn = cn_sc[h]

        logd = jnp.where(causal, bc - br + igr, neg_inf)
        inter = bc + m_prev
        m_t = jnp.maximum(inter, jnp.max(logd, axis=1, keepdims=True))
        sc = jnp.dot(q_hb, kt_h.astype(BF16), preferred_element_type=F32) * jnp.exp(logd - m_t)
        w_inter = jnp.exp(inter - m_t)
        intra = jnp.dot(sc.astype(BF16), v1_h, preferred_element_type=F32)
        cross = jnp.dot(q_hb, cn.astype(BF16), preferred_element_type=F32)
        num = intra[:, :L] + w_inter * cross[:, :L]
        den = intra[:, L:L + 1] + w_inter * cross[:, L:L + 1]
        hs = num / jnp.maximum(jnp.abs(den), jnp.exp(-m_t))

        b_last = br[:, L - 1:L]
        logw = b_last - br + igr
        m_new = jnp.maximum(b_last + m_prev, jnp.max(logw, axis=1, keepdims=True))
        wk = jnp.exp(logw - m_new)
        decay = jnp.exp(b_last + m_prev - m_new)
        upd = jnp.dot((kt_h * wk).astype(BF16), v1_h, preferred_element_type=F32)
        cn_sc[h] = decay * cn + upd
        m_sc[h:h + 1, :] = jnp.broadcast_to(m_new, (1, L))

        o_gate = jax.nn.sigmoid(cvo[:, MLSTM_WIDTH + h * L:MLSTM_WIDTH + (h + 1) * L].astype(F32))
        hn = _rms(hs) * hgain_ref[:, sl]
        c0 = GMLP_WIDTH + ATTN_WIDTH
        out_ref[:, c0 + h * L:c0 + (h + 1) * L] = (o_gate * hn).astype(BF16)


def _mixers(proj, tables, params, batch, seq):
    uv, q, kv, cqk, cvo, gates = proj
    rc, rs1, rs2 = tables
    vgain, ws, bst, sinks, convw, gbias, hgain = params
    nc = seq // CHUNK
    L = CHUNK

    def tok(width):
        return pl.BlockSpec((L, width), lambda b, n: (b * nc + n, 0))

    def const(shape):
        return pl.BlockSpec(shape, lambda b, n: (0,) * len(shape))

    rot_spec = pl.BlockSpec((L, LANES), lambda b, n: (n, 0))
    return pl.pallas_call(
        _mixer_kernel,
        grid=(batch, nc),
        in_specs=[
            tok(uv.shape[1]), tok(q.shape[1]), tok(kv.shape[1]), tok(cqk.shape[1]), tok(cvo.shape[1]),
            tok(LANES), rot_spec, rot_spec, rot_spec,
            const(vgain.shape), const(ws.shape), const(bst.shape),
            pl.BlockSpec(memory_space=pltpu.SMEM),
            const(convw.shape), const(gbias.shape), const(hgain.shape),
        ],
        out_specs=pl.BlockSpec((L, D_MODEL), lambda b, n: (b * nc + n, 0)),
        out_shape=jax.ShapeDtypeStruct((batch * seq, D_MODEL), BF16),
        scratch_shapes=[
            pltpu.VMEM((GMLP_GROUPS, L, L), BF16),
            pltpu.VMEM((GMLP_GROUPS, L, L), F32),
            pltpu.VMEM((N_KV_HEADS, 4 * L, L), BF16),
            pltpu.VMEM((N_KV_HEADS, 4 * L, 2 * L), BF16),
            pltpu.VMEM((8 + L, 2 * MLSTM_WIDTH), F32),
            pltpu.VMEM((MLSTM_HEADS, MLSTM_HEAD_DIM, 2 * L), F32),
            pltpu.VMEM((8, L), F32),
        ],
        compiler_params=pltpu.CompilerParams(
            dimension_semantics=("arbitrary", "arbitrary"), vmem_limit_bytes=VMEM_LIMIT_BYTES),
        name="mixers",
    )(uv, q, kv, cqk, cvo, gates, rc, rs1, rs2, vgain, ws, bst, sinks, convw, gbias, hgain)


def _out_ffn_kernel(x_ref, mix_ref, wout_ref, g2_ref, wg_ref, wu_ref, wd_ref, fg_ref, y_ref, hn_sc,
                    *, final_norm):
    k = pl.program_id(1)

    @pl.when(k == 0)
    def _():
        h = x_ref[...] + jnp.dot(mix_ref[...], wout_ref[...], preferred_element_type=F32)
        y_ref[...] = h
        hn_sc[...] = (_rms(h) * g2_ref[...]).astype(BF16)

    hn = hn_sc[...]
    gate = jnp.dot(hn, wg_ref[...], preferred_element_type=F32)
    up = jnp.dot(hn, wu_ref[...], preferred_element_type=F32)
    act = (gate * jax.nn.sigmoid(gate) * up).astype(BF16)
    y_ref[...] += jnp.dot(act, wd_ref[...], preferred_element_type=F32)

    if final_norm:
        @pl.when(k == pl.num_programs(1) - 1)
        def _():
            y_ref[...] = _rms(y_ref[...]) * fg_ref[...]


def _out_ffn(x2d, mixed, w_out, g2, w_gate, w_up, w_down, final_g, tm, tf, final_norm):
    m = x2d.shape[0]
    return pl.pallas_call(
        functools.partial(_out_ffn_kernel, final_norm=final_norm),
        grid=(m // tm, D_FF // tf),
        in_specs=[
            pl.BlockSpec((tm, D_MODEL), lambda i, k: (i, 0)),
            pl.BlockSpec((tm, D_MODEL), lambda i, k: (i, 0)),
            pl.BlockSpec((D_MODEL, D_MODEL), lambda i, k: (0, 0), pipeline_mode=pl.Buffered(1)),
            pl.BlockSpec((1, D_MODEL), lambda i, k: (0, 0)),
            pl.BlockSpec((D_MODEL, tf), lambda i, k: (0, k)),
            pl.BlockSpec((D_MODEL, tf), lambda i, k: (0, k)),
            pl.BlockSpec((tf, D_MODEL), lambda i, k: (k, 0)),
            pl.BlockSpec((1, D_MODEL), lambda i, k: (0, 0)),
        ],
        out_specs=pl.BlockSpec((tm, D_MODEL), lambda i, k: (i, 0)),
        out_shape=jax.ShapeDtypeStruct((m, D_MODEL), F32),
        scratch_shapes=[pltpu.VMEM((tm, D_MODEL), BF16)],
        compiler_params=pltpu.CompilerParams(
            dimension_semantics=("arbitrary", "arbitrary"), vmem_limit_bytes=VMEM_LIMIT_BYTES),
        name="out_ffn_final" if final_norm else "out_ffn",
    )(x2d, mixed, w_out, g2, w_gate, w_up, w_down, final_g)


def _rotary_tables(seq):
    half = ROT_DIM // 2
    inv_freq = ROPE_THETA ** (-jnp.arange(0, ROT_DIM, 2, dtype=F32) / ROT_DIM)
    ang = jnp.arange(seq, dtype=F32)[:, None] * inv_freq[None, :]
    cos, sin = jnp.cos(ang), jnp.sin(ang)
    ones = jnp.ones((seq, HEAD_DIM - ROT_DIM), F32)
    rc = jnp.concatenate([cos, cos, ones], axis=1)
    rs1 = jnp.concatenate([-sin, jnp.zeros((seq, HEAD_DIM - half), F32)], axis=1)
    rs2 = jnp.concatenate([jnp.zeros((seq, half), F32), sin, jnp.zeros((seq, HEAD_DIM - ROT_DIM), F32)], axis=1)
    reps = LANES // HEAD_DIM
    return tuple(jnp.tile(t, (1, reps)) for t in (rc, rs1, rs2))


def _layer(h2d, tables, batch, seq, norm1_g, w_in, gmlp_v_gain, gmlp_w_s, gmlp_b_s, attn_sinks,
           mlstm_conv_w, mlstm_i_bias, mlstm_f_bias, mlstm_head_gain, w_out, norm2_g, w_gate, w_up,
           w_down, final_g, final_norm):
    m = h2d.shape[0]
    tm = min(512, m)
    w_in_pad = jnp.pad(w_in.astype(BF16), ((0, 0), (0, IN_PROJ_PADDED - IN_PROJ_WIDTH)))
    proj = _in_proj(h2d, norm1_g.reshape(1, D_MODEL), w_in_pad, tm)
    gbias = jnp.pad(jnp.concatenate([mlstm_i_bias, mlstm_f_bias]), (0, LANES - 2 * MLSTM_HEADS))
    params = (
        gmlp_v_gain.reshape(1, GMLP_WIDTH), gmlp_w_s, gmlp_b_s.T, attn_sinks, mlstm_conv_w,
        gbias.reshape(1, LANES), mlstm_head_gain.reshape(1, MLSTM_WIDTH),
    )
    mixed = _mixers(proj, tables, params, batch, seq)
    return _out_ffn(h2d, mixed, w_out.astype(BF16), norm2_g.reshape(1, D_MODEL), w_gate.astype(BF16),
                    w_up.astype(BF16), w_down.astype(BF16), final_g.reshape(1, D_MODEL), tm, 512, final_norm)


def kernel(x, norm1_g, w_in, gmlp_v_gain, gmlp_w_s, gmlp_b_s, attn_sinks, mlstm_conv_w, mlstm_i_bias, mlstm_f_bias, mlstm_head_gain, w_out, norm2_g, w_gate, w_up, w_down, final_g):
    batch, seq, _ = x.shape
    depth = w_in.shape[0]
    tables = _rotary_tables(seq)
    h = x.reshape(batch * seq, D_MODEL)
    for l in range(depth):
        h = _layer(h, tables, batch, seq, norm1_g[l], w_in[l], gmlp_v_gain[l], gmlp_w_s[l], gmlp_b_s[l],
                   attn_sinks[l], mlstm_conv_w[l], mlstm_i_bias[l], mlstm_f_bias[l], mlstm_head_gain[l],
                   w_out[l], norm2_g[l], w_gate[l], w_up[l], w_down[l], final_g, l == depth - 1)
    return h.reshape(batch, seq, D_MODEL)
```

```python
import functools
import math

import jax
import jax.numpy as jnp
from jax import lax
from jax.experimental import pallas as pl
from jax.experimental.pallas import tpu as pltpu

F32 = jnp.float32
BF16 = jnp.bfloat16

D_MODEL = 2048
EPS = 1e-6
CHUNK = 128
GMLP_WIDTH = D_MODEL // 4
GMLP_GROUPS = 4
ATTN_WIDTH = D_MODEL // 2
HEAD_DIM = 64
N_Q_HEADS = ATTN_WIDTH // HEAD_DIM
N_KV_HEADS = 2
ROT_DIM = 16
ROPE_THETA = 500000.0
MLSTM_WIDTH = D_MODEL - GMLP_WIDTH - ATTN_WIDTH
MLSTM_HEADS = 4
MLSTM_HEAD_DIM = MLSTM_WIDTH // MLSTM_HEADS
CONV_WIDTH = 4
D_FF = 5632
LOG2E = math.log2(math.e)

OFF_UV = 0
OFF_Q = 2 * GMLP_WIDTH
OFF_KV = OFF_Q + ATTN_WIDTH
OFF_CQK = OFF_KV + 2 * N_KV_HEADS * HEAD_DIM
OFF_CVO = OFF_CQK + 2 * MLSTM_WIDTH
OFF_GATES = OFF_CVO + 2 * MLSTM_WIDTH
IN_PROJ_WIDTH = OFF_GATES + 2 * MLSTM_HEADS
LANES = 128
SUBLANES = 8
IN_PROJ_PADDED = OFF_GATES + LANES

VMEM_LIMIT_BYTES = 56 * 1024 * 1024


def _rms(x):
    return x * lax.rsqrt(jnp.mean(x * x, axis=-1, keepdims=True) + EPS)


def _rotary(xs, rc, rs1, rs2):
    half = ROT_DIM // 2
    return xs * rc + pltpu.roll(xs, LANES - half, 1) * rs1 + pltpu.roll(xs, half, 1) * rs2


def _in_proj_kernel(x_ref, g_ref, w_ref, rc_ref, rs1_ref, rs2_ref, vgain_ref, convw_ref, gbias_ref,
                    uv_ref, q_ref, kv_ref, cqk_ref, cvo_ref, gates_ref, conv_sc, raw_sc, *, tiles_per_seq):
    tm = x_ref.shape[0]
    hn = (_rms(x_ref[...]) * g_ref[...]).astype(BF16)

    def mm(lo, hi):
        return jnp.dot(hn, w_ref[:, lo:hi], preferred_element_type=F32)

    @pl.when(pl.program_id(0) % tiles_per_seq == 0)
    def _():
        conv_sc[0:SUBLANES, :] = jnp.zeros((SUBLANES, 2 * MLSTM_WIDTH), F32)

    conv_sc[SUBLANES:SUBLANES + tm, :] = mm(OFF_CQK, OFF_CVO)
    for lo, hi in ((OFF_UV, OFF_Q), (OFF_Q, OFF_KV), (OFF_CVO, OFF_GATES), (OFF_KV, OFF_CQK),
                   (OFF_GATES, IN_PROJ_PADDED)):
        raw_sc[:, lo:hi] = mm(lo, hi)

    base = SUBLANES - (CONV_WIDTH - 1)
    for cb in range(2 * MLSTM_WIDTH // LANES):
        cols = slice(cb * LANES, (cb + 1) * LANES)
        xv = conv_sc[:, cols]
        conv = xv[base:base + tm] * convw_ref[0:1, cols]
        for t in range(1, CONV_WIDTH):
            conv = conv + xv[base + t:base + t + tm] * convw_ref[t:t + 1, cols]
        act = conv * jax.nn.sigmoid(conv)
        if cb < MLSTM_WIDTH // LANES:
            act = act * (MLSTM_HEAD_DIM ** -0.5)
        cqk_ref[:, cols] = act.astype(BF16)
    conv_sc[0:SUBLANES, :] = conv_sc[tm:tm + SUBLANES, :]

    uv_ref[:, :GMLP_WIDTH] = jax.nn.gelu(raw_sc[:, OFF_UV:OFF_UV + GMLP_WIDTH]).astype(BF16)
    uv_ref[:, GMLP_WIDTH:] = (_rms(jax.nn.gelu(raw_sc[:, OFF_UV + GMLP_WIDTH:OFF_Q]))
                              * vgain_ref[...]).astype(BF16)

    rc = rc_ref[...]
    rs1 = rs1_ref[...]
    rs2 = rs2_ref[...]
    q_scale = (HEAD_DIM ** -0.5) * LOG2E
    for j in range(ATTN_WIDTH // LANES):
        q_ref[:, j * LANES:(j + 1) * LANES] = (
            _rotary(raw_sc[:, OFF_Q + j * LANES:OFF_Q + (j + 1) * LANES], rc, rs1, rs2) * q_scale).astype(BF16)
    kv_ref[:, :LANES] = _rotary(raw_sc[:, OFF_KV:OFF_KV + LANES], rc, rs1, rs2).astype(BF16)
    kv_ref[:, LANES:] = raw_sc[:, OFF_KV + LANES:OFF_CQK].astype(BF16)

    cvo_ref[:, :MLSTM_WIDTH] = raw_sc[:, OFF_CVO:OFF_CVO + MLSTM_WIDTH].astype(BF16)
    cvo_ref[:, MLSTM_WIDTH:] = jax.nn.sigmoid(raw_sc[:, OFF_CVO + MLSTM_WIDTH:OFF_GATES]).astype(BF16)

    gts = raw_sc[:, OFF_GATES:IN_PROJ_PADDED] + gbias_ref[...]
    lane = lax.broadcasted_iota(jnp.int32, gts.shape, 1)
    is_f = (lane >= MLSTM_HEADS) & (lane < 2 * MLSTM_HEADS)
    gates_ref[...] = jnp.where(is_f, jax.nn.log_sigmoid(gts), gts)


def _in_proj(x2d, g, w_pad, tables, vgain, convw, gbias, tm, seq):
    m = x2d.shape[0]
    assert seq % tm == 0 and m % tm == 0
    tiles_per_seq = seq // tm
    widths = (OFF_Q - OFF_UV, OFF_KV - OFF_Q, OFF_CQK - OFF_KV, OFF_CVO - OFF_CQK, OFF_GATES - OFF_CVO)
    out_shape = [jax.ShapeDtypeStruct((m, w), BF16) for w in widths]
    out_shape.append(jax.ShapeDtypeStruct((m, LANES), F32))
    out_specs = [pl.BlockSpec((tm, w), lambda i: (i, 0)) for w in widths]
    out_specs.append(pl.BlockSpec((tm, LANES), lambda i: (i, 0)))
    rot_spec = pl.BlockSpec((tm, LANES), lambda i: (i % tiles_per_seq, 0))

    def const(a):
        return pl.BlockSpec(a.shape, lambda i: (0,) * a.ndim)

    return pl.pallas_call(
        functools.partial(_in_proj_kernel, tiles_per_seq=tiles_per_seq),
        grid=(m // tm,),
        in_specs=[
            pl.BlockSpec((tm, D_MODEL), lambda i: (i, 0)),
            const(g),
            pl.BlockSpec((D_MODEL, IN_PROJ_PADDED), lambda i: (0, 0), pipeline_mode=pl.Buffered(1)),
            rot_spec, rot_spec, rot_spec,
            const(vgain), const(convw), const(gbias),
        ],
        out_specs=out_specs,
        out_shape=out_shape,
        scratch_shapes=[
            pltpu.VMEM((SUBLANES + tm, 2 * MLSTM_WIDTH), F32),
            pltpu.VMEM((tm, IN_PROJ_PADDED), F32),
        ],
        compiler_params=pltpu.CompilerParams(
            dimension_semantics=("arbitrary",), vmem_limit_bytes=VMEM_LIMIT_BYTES),
        name="in_proj",
    )(x2d, g, w_pad, *tables, vgain, convw, gbias)


def _mixer_kernel(uv_ref, q_ref, kv_ref, cqk_ref, cvo_ref, gates_ref,
                  ws_ref, bst_ref, sinks_ref, hgain_ref,
                  out_ref,
                  wtril_sc, bfull_sc, bdk_sc, bdv_sc, cn_sc, m_sc, s_sc, mx_sc, p_sc):
    n = pl.program_id(1)
    nb = uv_ref.shape[0]
    L = CHUNK
    row = lax.broadcasted_iota(jnp.int32, (L, L), 0)
    col = lax.broadcasted_iota(jnp.int32, (L, L), 1)
    causal = col <= row
    lo_half = col < HEAD_DIM
    neg_inf = jnp.float32(-jnp.inf)

    @pl.when(n == 0)
    def _start_of_sequence():
        for g in range(GMLP_GROUPS):
            wtril_sc[g] = jnp.where(causal, ws_ref[g], 0.0).astype(BF16)
            bfull_sc[g] = jnp.broadcast_to(bst_ref[:, g:g + 1], (L, L))
        r4 = lax.broadcasted_iota(jnp.int32, (4 * L, 2 * L), 0)
        c4 = lax.broadcasted_iota(jnp.int32, (4 * L, 2 * L), 1)
        ones_a = (c4 >= 2 * HEAD_DIM) & (c4 < 3 * HEAD_DIM) & (r4 < 2 * L)
        ones_b = (c4 >= 3 * HEAD_DIM) & (r4 >= 2 * L)
        ones_pat = jnp.where(ones_a | ones_b, 1.0, 0.0).astype(BF16)
        for b in range(nb):
            for h in range(N_KV_HEADS):
                bdk_sc[b, h] = jnp.zeros((4 * L, L), BF16)
                bdv_sc[b, h] = ones_pat
        cn_sc[...] = jnp.zeros(cn_sc.shape, F32)
        m_sc[...] = jnp.zeros(m_sc.shape, F32)

    qi = lax.broadcasted_iota(jnp.int32, (L, 2 * L), 0)
    kj = lax.broadcasted_iota(jnp.int32, (L, 2 * L), 1)
    band = (kj > qi) & (kj <= qi + L) & ((kj >= L) | (n > 0))
    key0 = lax.broadcasted_iota(jnp.int32, (1, 2 * L), 1) == 0
    lane8 = lax.broadcasted_iota(jnp.int32, (SUBLANES, L), 1)
    ones_blk = jnp.ones((L, L), BF16)
    zero = jnp.zeros((L, L), F32)
    slabs_per_kv = N_Q_HEADS // N_KV_HEADS // 2

    rows = range(nb)

    g_rows, c_rows, c_cols = [], [], []
    for b in rows:
        gr = gates_ref[b].T[0:SUBLANES, :]
        cr = gr
        shift = 1
        while shift < L:
            cr = cr + jnp.where(lane8 >= shift, pltpu.roll(cr, shift, 1), 0.0)
            shift *= 2
        g_rows.append(gr)
        c_rows.append(cr)
        c_cols.append(jnp.concatenate([cr, jnp.zeros((L - SUBLANES, L), F32)], axis=0).T)

    for b in rows:
        for g in range(GMLP_GROUPS):
            sl = slice(g * L, (g + 1) * L)
            mixed = jnp.dot(wtril_sc[g], uv_ref[b, :, GMLP_WIDTH + g * L:GMLP_WIDTH + (g + 1) * L],
                            preferred_element_type=F32) + bfull_sc[g]
            out_ref[b, :, sl] = (uv_ref[b, :, sl].astype(F32) * mixed).astype(BF16)

    for b in rows:
        for h in range(N_KV_HEADS):
            bdk_sc[b, h, 0:L, :] = bdk_sc[b, h, L:2 * L, :]
            bdk_sc[b, h, 2 * L:3 * L, :] = bdk_sc[b, h, 3 * L:4 * L, :]
            for base in (0, 2 * L):
                cur_v = bdv_sc[b, h, base + L:base + 2 * L, 0:L].astype(F32)
                bdv_sc[b, h, base:base + L, 0:L] = jnp.where(row == 0, zero, cur_v).astype(BF16)
        kvf = kv_ref[b].astype(F32)
        kr = kvf[:, :L]
        vf = kvf[:, L:]
        kroll = pltpu.roll(kr, HEAD_DIM, 1)
        vroll = pltpu.roll(vf, HEAD_DIM, 1)
        bdk_sc[b, 0, L:2 * L, :] = jnp.where(lo_half, kr, zero).astype(BF16)
        bdk_sc[b, 0, 3 * L:4 * L, :] = jnp.where(lo_half, zero, kroll).astype(BF16)
        bdk_sc[b, 1, L:2 * L, :] = jnp.where(lo_half, kroll, zero).astype(BF16)
        bdk_sc[b, 1, 3 * L:4 * L, :] = jnp.where(lo_half, zero, kr).astype(BF16)
        bdv_sc[b, 0, L:2 * L, 0:L] = jnp.where(lo_half, vf, zero).astype(BF16)
        bdv_sc[b, 0, 3 * L:4 * L, 0:L] = jnp.where(lo_half, zero, vroll).astype(BF16)
        bdv_sc[b, 1, L:2 * L, 0:L] = jnp.where(lo_half, vroll, zero).astype(BF16)
        bdv_sc[b, 1, 3 * L:4 * L, 0:L] = jnp.where(lo_half, zero, vf).astype(BF16)

    def scores(b, h):
        q_h = jnp.concatenate(
            [q_ref[b, :, (slabs_per_kv * h + r) * L:(slabs_per_kv * h + r + 1) * L]
             for r in range(slabs_per_kv)], axis=0)
        return lax.dot_general(q_h, bdk_sc[b, h], (((1,), (1,)), ((), ())),
                               preferred_element_type=F32)

    def pass_max(slot, b, h):
        s_h = scores(b, h)
        for r in range(slabs_per_kv):
            j = slabs_per_kv * h + r
            for a in range(2):
                fill = jnp.where(key0, sinks_ref[2 * j + a] * LOG2E, neg_inf)
                s = jnp.where(band, s_h[r * L:(r + 1) * L, a * 2 * L:(a + 1) * 2 * L], fill)
                s_sc[slot, r * L:(r + 1) * L, a * 2 * L:(a + 1) * 2 * L] = s
                mx_sc[slot, r * L:(r + 1) * L, a * L:(a + 1) * L] = jnp.broadcast_to(
                    jnp.max(s, axis=1, keepdims=True), (L, L))

    def pass_exp(slot):
        for r in range(slabs_per_kv):
            for a in range(2):
                mx = mx_sc[slot, r * L:(r + 1) * L, a * L:(a + 1) * L]
                for c in range(2):
                    cols = slice((2 * a + c) * L, (2 * a + c + 1) * L)
                    p_sc[slot, r * L:(r + 1) * L, cols] = jnp.exp2(
                        s_sc[slot, r * L:(r + 1) * L, cols] - mx).astype(BF16)

    def pass_pv(slot, b, h):
        o_h = jnp.dot(p_sc[slot], bdv_sc[b, h], preferred_element_type=F32)
        for r in range(slabs_per_kv):
            j = slabs_per_kv * h + r
            o = o_h[r * L:(r + 1) * L, 0:L] / o_h[r * L:(r + 1) * L, L:2 * L]
            out_ref[b, :, GMLP_WIDTH + j * L:GMLP_WIDTH + (j + 1) * L] = o.astype(BF16)

    attn_chains = [(b, h) for b in rows for h in range(N_KV_HEADS)]
    pass_max(0, *attn_chains[0])
    for ci, (b, h) in enumerate(attn_chains):
        if ci + 1 < len(attn_chains):
            pass_max((ci + 1) % 2, *attn_chains[ci + 1])
        pass_exp(ci % 2)
        pass_pv(ci % 2, b, h)

    chains = [(b, h) for b in rows for h in range(MLSTM_HEADS)]
    st = {}
    for (b, h) in chains:
        sl = slice(h * L, (h + 1) * L)
        bc = jnp.broadcast_to(c_cols[b][:, MLSTM_HEADS + h:MLSTM_HEADS + h + 1], (L, L))
        br = c_rows[b][MLSTM_HEADS + h:MLSTM_HEADS + h + 1, :]
        igr = g_rows[b][h:h + 1, :]
        m_prev = m_sc[b, h:h + 1, 0:1]
        logd = jnp.where(causal, bc - br + igr, neg_inf)
        inter = bc + m_prev
        m_t = jnp.maximum(inter, jnp.max(logd, axis=1, keepdims=True))
        b_last = br[:, L - 1:L]
        logw = b_last - br + igr
        m_new = jnp.maximum(b_last + m_prev, jnp.max(logw, axis=1, keepdims=True))
        kt = cqk_ref[b, :, MLSTM_WIDTH + h * L:MLSTM_WIDTH + (h + 1) * L].astype(F32).T
        st[b, h] = dict(
            dmat=jnp.exp(logd - m_t), m_t=m_t, w_inter=jnp.exp(inter - m_t), m_new=m_new,
            decay=jnp.exp(b_last + m_prev - m_new), kt=kt.astype(BF16),
            ktw=(kt * jnp.exp(logw - m_new)).astype(BF16),
            v1=jnp.concatenate([cvo_ref[b, :, sl], ones_blk], axis=1),
            cn=cn_sc[b, h])
    for (b, h) in chains:
        c = st[b, h]
        q_hb = cqk_ref[b, :, h * L:(h + 1) * L]
        c["sc"] = (jnp.dot(q_hb, c["kt"], preferred_element_type=F32) * c["dmat"]).astype(BF16)
        cross = jnp.dot(q_hb, c["cn"].astype(BF16), preferred_element_type=F32)
        c["wc"] = jnp.concatenate([c["w_inter"] * cross[:, :L], c["w_inter"] * cross[:, L:]], axis=1)
        cn_sc[b, h] = c["decay"] * c["cn"] + jnp.dot(c["ktw"], c["v1"], preferred_element_type=F32)
        m_sc[b, h:h + 1, :] = jnp.broadcast_to(c["m_new"], (1, L))
    for (b, h) in chains:
        c = st[b, h]
        sl = slice(h * L, (h + 1) * L)
        tot = jnp.dot(c["sc"], c["v1"], preferred_element_type=F32) + c["wc"]
        hs = tot[:, :L] / jnp.maximum(jnp.abs(tot[:, L:]), jnp.exp(-c["m_t"]))
        o_gate = cvo_ref[b, :, MLSTM_WIDTH + h * L:MLSTM_WIDTH + (h + 1) * L].astype(F32)
        c0 = GMLP_WIDTH + ATTN_WIDTH
        out_ref[b, :, c0 + h * L:c0 + (h + 1) * L] = (o_gate * (_rms(hs) * hgain_ref[:, sl])).astype(BF16)


def _mixers(proj, params, batch, seq, nb):
    uv, q, kv, cqk, cvo, gates = (a.reshape(batch, seq, a.shape[-1]) for a in proj)
    ws, bst, sinks, hgain = params
    nc = seq // CHUNK
    L = CHUNK
    assert batch % nb == 0

    def tok(a):
        return pl.BlockSpec((nb, L, a.shape[-1]), lambda b, n: (b, n, 0))

    def const(a):
        return pl.BlockSpec(a.shape, lambda b, n: (0,) * a.ndim)

    out = pl.pallas_call(
        _mixer_kernel,
        grid=(batch // nb, nc),
        in_specs=[
            tok(uv), tok(q), tok(kv), tok(cqk), tok(cvo), tok(gates),
            const(ws), const(bst), pl.BlockSpec(memory_space=pltpu.SMEM), const(hgain),
        ],
        out_specs=pl.BlockSpec((nb, L, D_MODEL), lambda b, n: (b, n, 0)),
        out_shape=jax.ShapeDtypeStruct((batch, seq, D_MODEL), BF16),
        scratch_shapes=[
            pltpu.VMEM((GMLP_GROUPS, L, L), BF16),
            pltpu.VMEM((GMLP_GROUPS, L, L), F32),
            pltpu.VMEM((nb, N_KV_HEADS, 4 * L, L), BF16),
            pltpu.VMEM((nb, N_KV_HEADS, 4 * L, 2 * L), BF16),
            pltpu.VMEM((nb, MLSTM_HEADS, MLSTM_HEAD_DIM, 2 * L), F32),
            pltpu.VMEM((nb, SUBLANES, L), F32),
            pltpu.VMEM((2, 4 * L, 4 * L), F32),
            pltpu.VMEM((2, 4 * L, 2 * L), F32),
            pltpu.VMEM((2, 4 * L, 4 * L), BF16),
        ],
        compiler_params=pltpu.CompilerParams(
            dimension_semantics=("arbitrary", "arbitrary"), vmem_limit_bytes=VMEM_LIMIT_BYTES),
        name="mixers",
    )(uv, q, kv, cqk, cvo, gates, ws, bst, sinks, hgain)
    return out.reshape(batch * seq, D_MODEL)


def _out_ffn_kernel(x_ref, mix_ref, wout_ref, g2_ref, wg_ref, wu_ref, wd_ref, fg_ref, y_ref, hn_sc,
                    *, final_norm):
    k = pl.program_id(1)

    @pl.when(k == 0)
    def _():
        h = x_ref[...] + jnp.dot(mix_ref[...], wout_ref[...], preferred_element_type=F32)
        y_ref[...] = h
        hn_sc[...] = (_rms(h) * g2_ref[...]).astype(BF16)

    hn = hn_sc[...]
    gate = jnp.dot(hn, wg_ref[...], preferred_element_type=F32)
    up = jnp.dot(hn, wu_ref[...], preferred_element_type=F32)
    act = (gate * jax.nn.sigmoid(gate) * up).astype(BF16)
    y_ref[...] += jnp.dot(act, wd_ref[...], preferred_element_type=F32)

    if final_norm:
        @pl.when(k == pl.num_programs(1) - 1)
        def _():
            y_ref[...] = _rms(y_ref[...]) * fg_ref[...]


def _out_ffn(x2d, mixed, w_out, g2, w_gate, w_up, w_down, final_g, tm, tf, final_norm):
    m = x2d.shape[0]
    return pl.pallas_call(
        functools.partial(_out_ffn_kernel, final_norm=final_norm),
        grid=(m // tm, D_FF // tf),
        in_specs=[
            pl.BlockSpec((tm, D_MODEL), lambda i, k: (i, 0)),
            pl.BlockSpec((tm, D_MODEL), lambda i, k: (i, 0)),
            pl.BlockSpec((D_MODEL, D_MODEL), lambda i, k: (0, 0), pipeline_mode=pl.Buffered(1)),
            pl.BlockSpec((1, D_MODEL), lambda i, k: (0, 0)),
            pl.BlockSpec((D_MODEL, tf), lambda i, k: (0, k)),
            pl.BlockSpec((D_MODEL, tf), lambda i, k: (0, k)),
            pl.BlockSpec((tf, D_MODEL), lambda i, k: (k, 0)),
            pl.BlockSpec((1, D_MODEL), lambda i, k: (0, 0)),
        ],
        out_specs=pl.BlockSpec((tm, D_MODEL), lambda i, k: (i, 0)),
        out_shape=jax.ShapeDtypeStruct((m, D_MODEL), F32),
        scratch_shapes=[pltpu.VMEM((tm, D_MODEL), BF16)],
        compiler_params=pltpu.CompilerParams(
            dimension_semantics=("arbitrary", "arbitrary"), vmem_limit_bytes=VMEM_LIMIT_BYTES),
        name="out_ffn_final" if final_norm else "out_ffn",
    )(x2d, mixed, w_out, g2, w_gate, w_up, w_down, final_g)


def _rotary_tables(seq):
    half = ROT_DIM // 2
    inv_freq = ROPE_THETA ** (-jnp.arange(0, ROT_DIM, 2, dtype=F32) / ROT_DIM)
    ang = jnp.arange(seq, dtype=F32)[:, None] * inv_freq[None, :]
    cos, sin = jnp.cos(ang), jnp.sin(ang)
    ones = jnp.ones((seq, HEAD_DIM - ROT_DIM), F32)
    rc = jnp.concatenate([cos, cos, ones], axis=1)
    rs1 = jnp.concatenate([-sin, jnp.zeros((seq, HEAD_DIM - half), F32)], axis=1)
    rs2 = jnp.concatenate([jnp.zeros((seq, half), F32), sin, jnp.zeros((seq, HEAD_DIM - ROT_DIM), F32)], axis=1)
    reps = LANES // HEAD_DIM
    return tuple(jnp.tile(t, (1, reps)) for t in (rc, rs1, rs2))


def _tiles(batch, seq):
    tm = min(512, seq)
    return tm, 512, batch


def _layer(h2d, tables, batch, seq, norm1_g, w_in, gmlp_v_gain, gmlp_w_s, gmlp_b_s, attn_sinks,
           mlstm_conv_w, mlstm_i_bias, mlstm_f_bias, mlstm_head_gain, w_out, norm2_g, w_gate, w_up,
           w_down, final_g, final_norm):
    tm, tf, nb = _tiles(batch, seq)
    w_in_pad = jnp.pad(w_in.astype(BF16), ((0, 0), (0, IN_PROJ_PADDED - IN_PROJ_WIDTH)))
    gbias = jnp.pad(jnp.concatenate([mlstm_i_bias, mlstm_f_bias]), (0, LANES - 2 * MLSTM_HEADS))
    proj = _in_proj(h2d, norm1_g.reshape(1, D_MODEL), w_in_pad, tables, gmlp_v_gain.reshape(1, GMLP_WIDTH),
                    mlstm_conv_w, gbias.reshape(1, LANES), tm, seq)
    params = (gmlp_w_s, gmlp_b_s.T, attn_sinks, mlstm_head_gain.reshape(1, MLSTM_WIDTH))
    mixed = _mixers(proj, params, batch, seq, nb)
    return _out_ffn(h2d, mixed, w_out.astype(BF16), norm2_g.reshape(1, D_MODEL), w_gate.astype(BF16),
                    w_up.astype(BF16), w_down.astype(BF16), final_g.reshape(1, D_MODEL), tm, tf, final_norm)


def kernel(x, norm1_g, w_in, gmlp_v_gain, gmlp_w_s, gmlp_b_s, attn_sinks, mlstm_conv_w, mlstm_i_bias, mlstm_f_bias, mlstm_head_gain, w_out, norm2_g, w_gate, w_up, w_down, final_g):
    batch, seq, _ = x.shape
    depth = w_in.shape[0]
    tables = _rotary_tables(seq)
    h = x.reshape(batch * seq, D_MODEL)
    for l in range(depth):
        h = _layer(h, tables, batch, seq, norm1_g[l], w_in[l], gmlp_v_gain[l], gmlp_w_s[l], gmlp_b_s[l],
                   attn_sinks[l], mlstm_conv_w[l], mlstm_i_bias[l], mlstm_f_bias[l], mlstm_head_gain[l],
                   w_out[l], norm2_g[l], w_gate[l], w_up[l], w_down[l], final_g, l == depth - 1)
    return h.reshape(batch, seq, D_MODEL)
```
